```python
import math
import jax, jax.numpy as jnp
from jax import lax
import numpy as np

D_MODEL = 1024
BATCH = 4
SEQ = 8192
DEPTH = 1

D_A = D_MODEL
D_B = D_MODEL
CONV_A_WIDTH = 3
CONV_B_WIDTH = 31
D_FF = 4 * D_MODEL
N_GROUPS = 16
LN_EPS = 1e-5
ALPHA = (2.0 * DEPTH) ** 0.25
BETA = (8.0 * DEPTH) ** -0.25
W_IN_COLS = 3 * D_A + 2 * D_B + 2 * D_MODEL

kernel_name = "hybrid_shortconv_conformer_gated_deepnorm"


def layernorm(x, gamma, beta):
    xf = x.astype(jnp.float32)
    mu = jnp.mean(xf, axis=-1, keepdims=True)
    var = jnp.mean(jnp.square(xf - mu), axis=-1, keepdims=True)
    y = (xf - mu) * lax.rsqrt(var + LN_EPS)
    y = y * gamma.astype(jnp.float32) + beta.astype(jnp.float32)
    return y.astype(x.dtype)


def causal_depthwise_conv(x, w):
    k = w.shape[0]
    c = x.shape[-1]
    return lax.conv_general_dilated(
        x, w[:, None, :].astype(x.dtype),
        window_strides=(1,),
        padding=[(k - 1, 0)],
        dimension_numbers=("NWC", "WIO", "NWC"),
        feature_group_count=c,
    )


def setup_inputs(seed: int = 0) -> dict:
    key = jax.random.key(seed)
    ks = jax.random.split(key, 20)
    f32 = jnp.float32
    nrm = lambda k, shape, scale: jax.random.normal(k, shape, f32) * scale
    return {
        "x": jax.random.normal(ks[0], (BATCH, SEQ, D_MODEL), f32),
        "w_in": nrm(ks[1], (D_MODEL, W_IN_COLS), D_MODEL ** -0.5),
        "conv_a_w": nrm(ks[2], (CONV_A_WIDTH, D_A), CONV_A_WIDTH ** -0.5),
        "w_out_a": nrm(ks[3], (D_A, D_MODEL), BETA * D_A ** -0.5),
        "conv_b_w": nrm(ks[4], (CONV_B_WIDTH, D_B), CONV_B_WIDTH ** -0.5),
        "conv_b_bias": nrm(ks[5], (D_B,), 0.02),
        "ln_b_gamma": 1.0 + nrm(ks[6], (D_B,), 0.02),
        "ln_b_beta": nrm(ks[7], (D_B,), 0.02),
        "w_out_b": nrm(ks[8], (D_B, D_MODEL), BETA * D_B ** -0.5),
        "w_o": nrm(ks[9], (D_MODEL, D_MODEL), BETA * D_MODEL ** -0.5),
        "ln1_gamma": 1.0 + nrm(ks[10], (D_MODEL,), 0.02),
        "ln1_beta": nrm(ks[11], (D_MODEL,), 0.02),
        "w_up": nrm(ks[12], (D_MODEL, D_FF), D_MODEL ** -0.5),
        "w_down": nrm(ks[13], (D_FF, D_MODEL), BETA * D_FF ** -0.5),
        "ln2_gamma": 1.0 + nrm(ks[14], (D_MODEL,), 0.02),
        "ln2_beta": nrm(ks[15], (D_MODEL,), 0.02),
    }


def token_mixer(x, w_in, conv_a_w, w_out_a, conv_b_w, conv_b_bias,
                ln_b_gamma, ln_b_beta, w_out_b, w_o):
    p = jnp.einsum("bsd,dc->bsc", x, w_in)
    splits = np.cumsum([D_A, D_A, D_A, D_B, D_B, D_MODEL])
    b_a, c_a, v_a, val_b, gate_b, g_a, g_b = jnp.split(p, splits, axis=-1)

    y_a = b_a * causal_depthwise_conv(c_a * v_a, conv_a_w)
    y_a = jnp.einsum("bsc,cd->bsd", y_a, w_out_a)

    u = val_b * jax.nn.sigmoid(gate_b)
    u = causal_depthwise_conv(u, conv_b_w) + conv_b_bias.astype(u.dtype)
    u = jax.nn.silu(layernorm(u, ln_b_gamma, ln_b_beta))
    y_b = jnp.einsum("bsc,cd->bsd", u, w_out_b)

    merged = jax.nn.sigmoid(g_a) * y_a + jax.nn.sigmoid(g_b) * y_b
    return jnp.einsum("bsd,de->bse", merged, w_o)


def channel_mixer(x, w_up, w_down):
    h = jnp.square(jax.nn.relu(jnp.einsum("bsd,df->bsf", x, w_up)))
    return jnp.einsum("bsf,fd->bsd", h, w_down)


def reference(x, w_in, conv_a_w, w_out_a, conv_b_w, conv_b_bias, ln_b_gamma,
              ln_b_beta, w_out_b, w_o, ln1_gamma, ln1_beta, w_up, w_down,
              ln2_gamma, ln2_beta):
    alpha = jnp.asarray(ALPHA, dtype=x.dtype)
    for _ in range(DEPTH):
        mix = token_mixer(x, w_in, conv_a_w, w_out_a, conv_b_w, conv_b_bias,
                          ln_b_gamma, ln_b_beta, w_out_b, w_o)
        x = layernorm(alpha * x + mix, ln1_gamma, ln1_beta)
        ff = channel_mixer(x, w_up, w_down)
        x = layernorm(alpha * x + ff, ln2_gamma, ln2_beta)
    return x
```

```python
import functools

import jax
import jax.numpy as jnp
from jax import lax
from jax.experimental import pallas as pl
from jax.experimental.pallas import tpu as pltpu

LN_EPS = 1e-5
ALPHA = 2.0 ** 0.25
CONV_A_WIDTH = 3
CONV_B_WIDTH = 31
N_PROJ = 7

SUBLANES = 8
LANES = 128
CV_HALO = 8
U_HALO = 32
CONV_ROWS = 128
VMEM_LIMIT_BYTES = 58 * 1024 * 1024


def _layernorm(y, gamma, beta):
    mu = jnp.mean(y, axis=-1, keepdims=True)
    d = y - mu
    var = jnp.mean(d * d, axis=-1, keepdims=True)
    return d * lax.rsqrt(var + LN_EPS) * gamma + beta


def _bf16_dot(a, w):
    return jnp.dot(a.astype(jnp.bfloat16), w, preferred_element_type=jnp.float32)


def _mixer_kernel(x_ref, w_in_ref, wa_ref, w_out_a_ref, wb8_ref, bias_b_ref, g_b_ref, b_b_ref,
                  w_out_b_ref, w_o_ref, g1_ref, b1_ref, o_ref, cv_ext, u_ext, conv_out):
    t, d = x_ref.shape
    i = pl.program_id(1)

    @pl.when(i == 0)
    def _():
        cv_ext[0:CV_HALO, :] = jnp.zeros((CV_HALO, d), jnp.float32)
        u_ext[0:U_HALO, :] = jnp.zeros((U_HALO, d), jnp.float32)

    @pl.when(i > 0)
    def _():
        cv_ext[0:CV_HALO, :] = cv_ext[t:t + CV_HALO, :]
        u_ext[0:U_HALO, :] = u_ext[t:t + U_HALO, :]

    x = x_ref[...]
    xb = x.astype(jnp.bfloat16)

    def proj(j):
        return jnp.dot(xb, w_in_ref[:, j * d:(j + 1) * d], preferred_element_type=jnp.float32)

    cv_ext[CV_HALO:CV_HALO + t, :] = proj(1) * proj(2)
    conv_a = None
    for k in range(CONV_A_WIDTH):
        off = CV_HALO - (CONV_A_WIDTH - 1) + k
        term = cv_ext[off:off + t, :] * wa_ref[k:k + 1, :]
        conv_a = term if conv_a is None else conv_a + term
    y_a = _bf16_dot(proj(0) * conv_a, w_out_a_ref[...])

    u_ext[U_HALO:U_HALO + t, :] = proj(3) * jax.nn.sigmoid(proj(4))

    n_lane_groups = d // LANES

    def conv_lane_group(lg, carry):
        c0 = pl.multiple_of(lg * LANES, LANES)
        for r0 in range(0, t, CONV_ROWS):
            acc = jnp.zeros((CONV_ROWS // SUBLANES, SUBLANES, LANES), jnp.float32)
            for k in range(CONV_B_WIDTH):
                off = r0 + U_HALO - (CONV_B_WIDTH - 1) + k
                u_k = u_ext[off:off + CONV_ROWS, pl.ds(c0, LANES)]
                w_k = wb8_ref[k * SUBLANES:(k + 1) * SUBLANES, pl.ds(c0, LANES)]
                acc = acc + u_k.reshape(CONV_ROWS // SUBLANES, SUBLANES, LANES) * w_k[None]
            conv_out[r0:r0 + CONV_ROWS, pl.ds(c0, LANES)] = acc.reshape(CONV_ROWS, LANES)
        return carry

    lax.fori_loop(0, n_lane_groups, conv_lane_group, 0)

    ub = _layernorm(conv_out[...] + bias_b_ref[...], g_b_ref[...], b_b_ref[...])
    y_b = _bf16_dot(ub * jax.nn.sigmoid(ub), w_out_b_ref[...])

    merged = jax.nn.sigmoid(proj(5)) * y_a + jax.nn.sigmoid(proj(6)) * y_b
    mix = _bf16_dot(merged, w_o_ref[...])
    o_ref[...] = _layernorm(ALPHA * x + mix, g1_ref[...], b1_ref[...])


def _mlp_kernel(x_ref, w_up_ref, w_down_ref, g2_ref, b2_ref, o_ref, *, ff_chunk):
    x = x_ref[...]
    xb = x.astype(jnp.bfloat16)
    d_ff = w_up_ref.shape[1]
    ff = None
    for c in range(d_ff // ff_chunk):
        cols = slice(c * ff_chunk, (c + 1) * ff_chunk)
        h = jnp.maximum(jnp.dot(xb, w_up_ref[:, cols], preferred_element_type=jnp.float32), 0.0)
        part = _bf16_dot(h * h, w_down_ref[cols, :])
        ff = part if ff is None else ff + part
    o_ref[...] = _layernorm(ALPHA * x + ff, g2_ref[...], b2_ref[...])


def _resident(shape):
    return pl.BlockSpec(shape, lambda *_: (0,) * len(shape), pipeline_mode=pl.Buffered(1))


def _seq_tile(s):
    for t in (256, 128):
        if s % t == 0:
            return t
    raise ValueError(f"sequence length {s} must be a multiple of {CONV_ROWS}")


def kernel(x, w_in, conv_a_w, w_out_a, conv_b_w, conv_b_bias, ln_b_gamma, ln_b_beta, w_out_b, w_o,
           ln1_gamma, ln1_beta, w_up, w_down, ln2_gamma, ln2_beta):
    b, s, d = x.shape
    d_ff = w_up.shape[1]
    assert w_in.shape == (d, N_PROJ * d) and conv_a_w.shape == (CONV_A_WIDTH, d)
    assert conv_b_w.shape == (CONV_B_WIDTH, d) and d % LANES == 0
    bf16 = jnp.bfloat16
    row = lambda v: v.reshape(1, d).astype(jnp.float32)
    t = _seq_tile(s)

    x1 = pl.pallas_call(
        _mixer_kernel,
        grid=(b, s // t),
        in_specs=[
            pl.BlockSpec((None, t, d), lambda bi, si: (bi, si, 0)),
            _resident((d, N_PROJ * d)),
            _resident((CONV_A_WIDTH, d)),
            _resident((d, d)),
            _resident((CONV_B_WIDTH * SUBLANES, d)),
            _resident((1, d)), _resident((1, d)), _resident((1, d)),
            _resident((d, d)),
            _resident((d, d)),
            _resident((1, d)), _resident((1, d)),
        ],
        out_specs=pl.BlockSpec((None, t, d), lambda bi, si: (bi, si, 0)),
        out_shape=jax.ShapeDtypeStruct((b, s, d), jnp.float32),
        scratch_shapes=[
            pltpu.VMEM((t + CV_HALO, d), jnp.float32),
            pltpu.VMEM((t + U_HALO, d), jnp.float32),
            pltpu.VMEM((t, d), jnp.float32),
        ],
        compiler_params=pltpu.CompilerParams(
            dimension_semantics=("arbitrary", "arbitrary"), vmem_limit_bytes=VMEM_LIMIT_BYTES),
        name="mixer",
    )(x, w_in.astype(bf16), conv_a_w, w_out_a.astype(bf16),
      jnp.repeat(conv_b_w, SUBLANES, axis=0), row(conv_b_bias), row(ln_b_gamma), row(ln_b_beta),
      w_out_b.astype(bf16), w_o.astype(bf16), row(ln1_gamma), row(ln1_beta))

    n = b * s
    tm = _seq_tile(n)
    out = pl.pallas_call(
        functools.partial(_mlp_kernel, ff_chunk=d),
        grid=(n // tm,),
        in_specs=[
            pl.BlockSpec((tm, d), lambda m: (m, 0)),
            _resident((d, d_ff)),
            _resident((d_ff, d)),
            _resident((1, d)), _resident((1, d)),
        ],
        out_specs=pl.BlockSpec((tm, d), lambda m: (m, 0)),
        out_shape=jax.ShapeDtypeStruct((n, d), jnp.float32),
        compiler_params=pltpu.CompilerParams(
            dimension_semantics=("arbitrary",), vmem_limit_bytes=VMEM_LIMIT_BYTES),
        name="mlp",
    )(x1.reshape(n, d), w_up.astype(bf16), w_down.astype(bf16), row(ln2_gamma), row(ln2_beta))
    return out.reshape(b, s, d)
```

```python
import functools

import jax
import jax.numpy as jnp
from jax import lax
from jax.experimental import pallas as pl
from jax.experimental.pallas import tpu as pltpu

LN_EPS = 1e-5
ALPHA = 2.0 ** 0.25
CONV_A_WIDTH = 3
CONV_B_WIDTH = 31
N_PROJ = 7

SUBLANES = 8
LANES = 128
CV_HALO = 8
U_HALO = 32
CONV_ROWS = 128
ROW_PHASES = 2
VMEM_LIMIT_BYTES = 58 * 1024 * 1024


def _layernorm(y, gamma, beta):
    mu = jnp.mean(y, axis=-1, keepdims=True)
    d = y - mu
    var = jnp.mean(d * d, axis=-1, keepdims=True)
    return d * lax.rsqrt(var + LN_EPS) * gamma + beta


def _pack_weight(w):
    k, n = w.shape
    u = lax.bitcast_convert_type(w.astype(jnp.bfloat16), jnp.uint16).astype(jnp.uint32).reshape(k // 2, 2, n)
    return u[:, 0, :] | (u[:, 1, :] << 16)


def _bf16_dot(a, w_u32):
    return jnp.dot(a.astype(jnp.bfloat16), pltpu.bitcast(w_u32, jnp.bfloat16), preferred_element_type=jnp.float32)


def _mixer_kernel(x_ref, w_in_ref, wa_ref, w_out_a_ref, wb8_ref, bias_b_ref, g_b_ref, b_b_ref,
                  w_out_b_ref, w_o_ref, g1_ref, b1_ref, o_ref, cv_ext, u_slab, conv_slab):
    t, d = x_ref.shape
    n_lane_groups = d // LANES
    i = pl.program_id(1)

    @pl.when(i == 0)
    def _():
        cv_ext[0:CV_HALO, :] = jnp.zeros((CV_HALO, d), jnp.float32)
        u_slab[:, 0:U_HALO, :] = jnp.zeros((n_lane_groups, U_HALO, LANES), jnp.float32)

    @pl.when(i > 0)
    def _():
        cv_ext[0:CV_HALO, :] = cv_ext[t:t + CV_HALO, :]
        u_slab[:, 0:U_HALO, :] = u_slab[:, t:t + U_HALO, :]

    x = x_ref[...]
    xb = x.astype(jnp.bfloat16)

    def proj(g):
        w = pltpu.bitcast(w_in_ref[:, g * d:(g + 1) * d], jnp.bfloat16)
        return jnp.dot(xb, w, preferred_element_type=jnp.float32)

    u = proj(3) * jax.nn.sigmoid(proj(4))
    for g in range(n_lane_groups):
        u_slab[g, U_HALO:U_HALO + t, :] = u[:, g * LANES:(g + 1) * LANES]

    def conv_lane_group(lg, carry):
        lanes = pl.ds(pl.multiple_of(lg * LANES, LANES), LANES)
        half = CONV_ROWS // ROW_PHASES
        for r0 in range(0, t, CONV_ROWS):
            acc = [None] * ROW_PHASES
            for k in range(CONV_B_WIDTH):
                w_k = wb8_ref[k * SUBLANES:(k + 1) * SUBLANES, lanes]
                for phase in range(ROW_PHASES):
                    start = r0 + U_HALO - (CONV_B_WIDTH - 1) + k + phase
                    u_k = u_slab[lg, pl.ds(start, half, stride=ROW_PHASES), :]
                    term = u_k.reshape(half // SUBLANES, SUBLANES, LANES) * w_k[None]
                    acc[phase] = term if acc[phase] is None else acc[phase] + term
            for phase in range(ROW_PHASES):
                conv_slab[lg, pl.ds(r0 + phase, half, stride=ROW_PHASES), :] = acc[phase].reshape(half, LANES)
        return carry

    lax.fori_loop(0, n_lane_groups, conv_lane_group, 0)

    cv_ext[CV_HALO:CV_HALO + t, :] = proj(1) * proj(2)
    conv_a = None
    for k in range(CONV_A_WIDTH):
        off = CV_HALO - (CONV_A_WIDTH - 1) + k
        term = cv_ext[off:off + t, :] * wa_ref[k:k + 1, :]
        conv_a = term if conv_a is None else conv_a + term
    lhs_a = proj(0) * conv_a

    conv = jnp.concatenate([conv_slab[g] for g in range(n_lane_groups)], axis=1)
    ub = _layernorm(conv + bias_b_ref[...], g_b_ref[...], b_b_ref[...])
    y_b = _bf16_dot(ub * jax.nn.sigmoid(ub), w_out_b_ref[...])
    y_a = _bf16_dot(lhs_a, w_out_a_ref[...])

    merged = jax.nn.sigmoid(proj(5)) * y_a + jax.nn.sigmoid(proj(6)) * y_b
    mix = _bf16_dot(merged, w_o_ref[...])
    o_ref[...] = _layernorm(ALPHA * x + mix, g1_ref[...], b1_ref[...])


def _mlp_kernel(x_ref, w_up_ref, w_down_ref, g2_ref, b2_ref, o_ref, *, ff_chunk):
    x = x_ref[...]
    xb = x.astype(jnp.bfloat16)
    d_ff = w_up_ref.shape[1]
    ff = None
    for c in range(d_ff // ff_chunk):
        w_up_c = pltpu.bitcast(w_up_ref[:, c * ff_chunk:(c + 1) * ff_chunk], jnp.bfloat16)
        h = jnp.maximum(jnp.dot(xb, w_up_c, preferred_element_type=jnp.float32), 0.0)
        part = _bf16_dot(h * h, w_down_ref[c * ff_chunk // 2:(c + 1) * ff_chunk // 2, :])
        ff = part if ff is None else ff + part
    o_ref[...] = _layernorm(ALPHA * x + ff, g2_ref[...], b2_ref[...])


def _resident(shape):
    return pl.BlockSpec(shape, lambda *_: (0,) * len(shape), pipeline_mode=pl.Buffered(1))


def _token_tile(n):
    for t in (256, 128):
        if n % t == 0:
            return t
    raise ValueError(f"token count {n} must be a multiple of {CONV_ROWS}")


def kernel(x, w_in, conv_a_w, w_out_a, conv_b_w, conv_b_bias, ln_b_gamma, ln_b_beta, w_out_b, w_o,
           ln1_gamma, ln1_beta, w_up, w_down, ln2_gamma, ln2_beta):
    b, s, d = x.shape
    d_ff = w_up.shape[1]
    assert w_in.shape == (d, N_PROJ * d) and conv_a_w.shape == (CONV_A_WIDTH, d)
    assert conv_b_w.shape == (CONV_B_WIDTH, d) and d % LANES == 0
    row = lambda v: v.reshape(1, d).astype(jnp.float32)
    t = _token_tile(s)

    x1 = pl.pallas_call(
        _mixer_kernel,
        grid=(b, s // t),
        in_specs=[
            pl.BlockSpec((None, t, d), lambda bi, si: (bi, si, 0)),
            _resident((d // 2, N_PROJ * d)),
            _resident((CONV_A_WIDTH, d)),
            _resident((d // 2, d)),
            _resident((CONV_B_WIDTH * SUBLANES, d)),
            _resident((1, d)), _resident((1, d)), _resident((1, d)),
            _resident((d // 2, d)),
            _resident((d // 2, d)),
            _resident((1, d)), _resident((1, d)),
        ],
        out_specs=pl.BlockSpec((None, t, d), lambda bi, si: (bi, si, 0)),
        out_shape=jax.ShapeDtypeStruct((b, s, d), jnp.float32),
        scratch_shapes=[
            pltpu.VMEM((t + CV_HALO, d), jnp.float32),
            pltpu.VMEM((d // LANES, t + U_HALO, LANES), jnp.float32),
            pltpu.VMEM((d // LANES, t, LANES), jnp.float32),
        ],
        compiler_params=pltpu.CompilerParams(
            dimension_semantics=("arbitrary", "arbitrary"), vmem_limit_bytes=VMEM_LIMIT_BYTES),
        name="mixer",
    )(x, _pack_weight(w_in), conv_a_w, _pack_weight(w_out_a),
      jnp.repeat(conv_b_w, SUBLANES, axis=0), row(conv_b_bias), row(ln_b_gamma), row(ln_b_beta),
      _pack_weight(w_out_b), _pack_weight(w_o), row(ln1_gamma), row(ln1_beta))

    n = b * s
    tm = _token_tile(n)
    out = pl.pallas_call(
        functools.partial(_mlp_kernel, ff_chunk=d),
        grid=(n // tm,),
        in_specs=[
            pl.BlockSpec((tm, d), lambda m: (m, 0)),
            _resident((d // 2, d_ff)),
            _resident((d_ff // 2, d)),
            _resident((1, d)), _resident((1, d)),
        ],
        out_specs=pl.BlockSpec((tm, d), lambda m: (m, 0)),
        out_shape=jax.ShapeDtypeStruct((n, d), jnp.float32),
        compiler_params=pltpu.CompilerParams(
            dimension_semantics=("arbitrary",), vmem_limit_bytes=VMEM_LIMIT_BYTES),
        name="mlp",
    )(x1.reshape(n, d), _pack_weight(w_up), _pack_weight(w_down), row(ln2_gamma), row(ln2_beta))
    return out.reshape(b, s, d)
```

```python
import functools

import jax
import jax.numpy as jnp
from jax import lax
from jax.experimental import pallas as pl
from jax.experimental.pallas import tpu as pltpu

LN_EPS = 1e-5
ALPHA = 2.0 ** 0.25
CONV_A_WIDTH = 3
CONV_B_WIDTH = 31
N_PROJ = 7

SUBLANES = 8
LANES = 128
CV_HALO = 8
U_HALO = 32
CONV_ROWS = 128
ROW_PHASES = 2
TAIL_ROWS = 256
MIXER_TILE = 512
MLP_TILE = 512
PACK_ROWS, PACK_COLS = 512, 1024
VMEM_LIMIT_BYTES = 58 * 1024 * 1024


def _layernorm(y, gamma, beta):
    mu = jnp.mean(y, axis=-1, keepdims=True)
    d = y - mu
    var = jnp.mean(d * d, axis=-1, keepdims=True)
    return d * lax.rsqrt(var + LN_EPS) * gamma + beta


def _pack_kernel(w_ref, o_ref):
    o_ref[...] = pltpu.bitcast(w_ref[...].astype(jnp.bfloat16), jnp.uint32)


def _pack_weight(w):
    k, n = w.shape
    rows = min(k, PACK_ROWS)
    cols = min(n, PACK_COLS)
    return pl.pallas_call(
        _pack_kernel,
        grid=(k // rows, n // cols),
        in_specs=[pl.BlockSpec((rows, cols), lambda i, j: (i, j))],
        out_specs=pl.BlockSpec((rows // 2, cols), lambda i, j: (i, j)),
        out_shape=jax.ShapeDtypeStruct((k // 2, n), jnp.uint32),
        name="pack_weight",
    )(w)


def _bf16_dot(a, w_u32):
    return jnp.dot(a.astype(jnp.bfloat16), pltpu.bitcast(w_u32, jnp.bfloat16), preferred_element_type=jnp.float32)


def _mixer_kernel(x_ref, w_in_ref, wa_ref, w_out_a_ref, wb8_ref, bias_b_ref, g_b_ref, b_b_ref,
                  w_out_b_ref, w_o_ref, g1_ref, b1_ref, o_ref, cv_ext, u_slab, conv_slab):
    t, d = x_ref.shape
    n_lane_groups = d // LANES
    i = pl.program_id(1)

    @pl.when(i == 0)
    def _():
        cv_ext[0:CV_HALO, :] = jnp.zeros((CV_HALO, d), jnp.float32)
        u_slab[:, 0:U_HALO, :] = jnp.zeros((n_lane_groups, U_HALO, LANES), jnp.float32)

    @pl.when(i > 0)
    def _():
        cv_ext[0:CV_HALO, :] = cv_ext[t:t + CV_HALO, :]
        u_slab[:, 0:U_HALO, :] = u_slab[:, t:t + U_HALO, :]

    x = x_ref[...]
    xb = x.astype(jnp.bfloat16)

    def proj(g):
        w = pltpu.bitcast(w_in_ref[:, g * d:(g + 1) * d], jnp.bfloat16)
        return jnp.dot(xb, w, preferred_element_type=jnp.float32)

    u = proj(3) * jax.nn.sigmoid(proj(4))
    for g in range(n_lane_groups):
        u_slab[g, U_HALO:U_HALO + t, :] = u[:, g * LANES:(g + 1) * LANES]

    def conv_lane_group(lg, carry):
        lanes = pl.ds(pl.multiple_of(lg * LANES, LANES), LANES)
        half = CONV_ROWS // ROW_PHASES
        for r0 in range(0, t, CONV_ROWS):
            acc = [None] * ROW_PHASES
            for k in range(CONV_B_WIDTH):
                w_k = wb8_ref[k * SUBLANES:(k + 1) * SUBLANES, lanes]
                for phase in range(ROW_PHASES):
                    start = r0 + U_HALO - (CONV_B_WIDTH - 1) + k + phase
                    u_k = u_slab[lg, pl.ds(start, half, stride=ROW_PHASES), :]
                    term = u_k.reshape(half // SUBLANES, SUBLANES, LANES) * w_k[None]
                    acc[phase] = term if acc[phase] is None else acc[phase] + term
            for phase in range(ROW_PHASES):
                conv_slab[lg, pl.ds(r0 + phase, half, stride=ROW_PHASES), :] = acc[phase].reshape(half, LANES)
        return carry

    lax.fori_loop(0, n_lane_groups, conv_lane_group, 0)

    cv_ext[CV_HALO:CV_HALO + t, :] = proj(1) * proj(2)
    conv_a = None
    for k in range(CONV_A_WIDTH):
        off = CV_HALO - (CONV_A_WIDTH - 1) + k
        term = cv_ext[off:off + t, :] * wa_ref[k:k + 1, :]
        conv_a = term if conv_a is None else conv_a + term
    lhs_a = proj(0) * conv_a

    y_a = _bf16_dot(lhs_a, w_out_a_ref[...])
    gate_a = jax.nn.sigmoid(proj(5)) * y_a
    gate_b = jax.nn.sigmoid(proj(6))

    for r0 in range(0, t, TAIL_ROWS):
        rows = slice(r0, min(r0 + TAIL_ROWS, t))
        conv = jnp.concatenate([conv_slab[g, rows, :] for g in range(n_lane_groups)], axis=1)
        ub = _layernorm(conv + bias_b_ref[...], g_b_ref[...], b_b_ref[...])
        y_b = _bf16_dot(ub * jax.nn.sigmoid(ub), w_out_b_ref[...])
        mix = _bf16_dot(gate_a[rows, :] + gate_b[rows, :] * y_b, w_o_ref[...])
        o_ref[rows, :] = _layernorm(ALPHA * x[rows, :] + mix, g1_ref[...], b1_ref[...])


def _mlp_kernel(x_ref, w_up_ref, w_down_ref, g2_ref, b2_ref, o_ref, *, ff_chunk):
    d_ff = w_up_ref.shape[1]
    t = x_ref.shape[0]
    for r0 in range(0, t, TAIL_ROWS):
        rows = slice(r0, min(r0 + TAIL_ROWS, t))
        x = x_ref[rows, :]
        xb = x.astype(jnp.bfloat16)
        ff = None
        for c in range(d_ff // ff_chunk):
            w_up_c = pltpu.bitcast(w_up_ref[:, c * ff_chunk:(c + 1) * ff_chunk], jnp.bfloat16)
            h = jnp.maximum(jnp.dot(xb, w_up_c, preferred_element_type=jnp.float32), 0.0)
            part = _bf16_dot(h * h, w_down_ref[c * ff_chunk // 2:(c + 1) * ff_chunk // 2, :])
            ff = part if ff is None else ff + part
        o_ref[rows, :] = _layernorm(ALPHA * x + ff, g2_ref[...], b2_ref[...])


def _resident(shape):
    return pl.BlockSpec(shape, lambda *_: (0,) * len(shape), pipeline_mode=pl.Buffered(1))


def _token_tile(n, largest):
    t = largest
    while t >= CONV_ROWS:
        if n % t == 0:
            return t
        t //= 2
    raise ValueError(f"token count {n} must be a multiple of {CONV_ROWS}")


def kernel(x, w_in, conv_a_w, w_out_a, conv_b_w, conv_b_bias, ln_b_gamma, ln_b_beta, w_out_b, w_o,
           ln1_gamma, ln1_beta, w_up, w_down, ln2_gamma, ln2_beta):
    b, s, d = x.shape
    d_ff = w_up.shape[1]
    assert w_in.shape == (d, N_PROJ * d) and conv_a_w.shape == (CONV_A_WIDTH, d)
    assert conv_b_w.shape == (CONV_B_WIDTH, d) and d % LANES == 0
    row = lambda v: v.reshape(1, d).astype(jnp.float32)
    t = _token_tile(s, MIXER_TILE)

    x1 = pl.pallas_call(
        _mixer_kernel,
        grid=(b, s // t),
        in_specs=[
            pl.BlockSpec((None, t, d), lambda bi, si: (bi, si, 0)),
            _resident((d // 2, N_PROJ * d)),
            _resident((CONV_A_WIDTH, d)),
            _resident((d // 2, d)),
            _resident((CONV_B_WIDTH * SUBLANES, d)),
            _resident((1, d)), _resident((1, d)), _resident((1, d)),
            _resident((d // 2, d)),
            _resident((d // 2, d)),
            _resident((1, d)), _resident((1, d)),
        ],
        out_specs=pl.BlockSpec((None, t, d), lambda bi, si: (bi, si, 0)),
        out_shape=jax.ShapeDtypeStruct((b, s, d), jnp.float32),
        scratch_shapes=[
            pltpu.VMEM((t + CV_HALO, d), jnp.float32),
            pltpu.VMEM((d // LANES, t + U_HALO, LANES), jnp.float32),
            pltpu.VMEM((d // LANES, t, LANES), jnp.float32),
        ],
        compiler_params=pltpu.CompilerParams(
            dimension_semantics=("arbitrary", "arbitrary"), vmem_limit_bytes=VMEM_LIMIT_BYTES),
        name="mixer",
    )(x, _pack_weight(w_in), conv_a_w, _pack_weight(w_out_a),
      jnp.repeat(conv_b_w, SUBLANES, axis=0), row(conv_b_bias), row(ln_b_gamma), row(ln_b_beta),
      _pack_weight(w_out_b), _pack_weight(w_o), row(ln1_gamma), row(ln1_beta))

    n = b * s
    tm = _token_tile(n, MLP_TILE)
    out = pl.pallas_call(
        functools.partial(_mlp_kernel, ff_chunk=d),
        grid=(n // tm,),
        in_specs=[
            pl.BlockSpec((tm, d), lambda m: (m, 0)),
            _resident((d // 2, d_ff)),
            _resident((d_ff // 2, d)),
            _resident((1, d)), _resident((1, d)),
        ],
        out_specs=pl.BlockSpec((tm, d), lambda m: (m, 0)),
        out_shape=jax.ShapeDtypeStruct((n, d), jnp.float32),
        compiler_params=pltpu.CompilerParams(
            dimension_semantics=("arbitrary",), vmem_limit_bytes=VMEM_LIMIT_BYTES),
        name="mlp",
    )(x1.reshape(n, d), _pack_weight(w_up), _pack_weight(w_down), row(ln2_gamma), row(ln2_beta))
    return out.reshape(b, s, d)
```

```python
import functools

import jax
import jax.numpy as jnp
from jax import lax
from jax.experimental import pallas as pl
from jax.experimental.pallas import tpu as pltpu

LN_EPS = 1e-5
ALPHA = 2.0 ** 0.25
CONV_A_WIDTH = 3
CONV_B_WIDTH = 31
N_PROJ = 7

SUBLANES = 8
LANES = 128
CV_HALO = 8
U_HALO = 32
CONV_ROWS = 128
ROW_PHASES = 2
TAIL_ROWS = 256
CONV_GROUPS = (2, 3, 3, 2, 2, 2, 2)
CONV_LAG = 1
MIXER_TILE = 256
MLP_TILE = 512
PACK_ROWS, PACK_COLS = 512, 1024
VMEM_LIMIT_BYTES = 58 * 1024 * 1024


def _layernorm(y, gamma, beta):
    mu = jnp.mean(y, axis=-1, keepdims=True)
    d = y - mu
    var = jnp.mean(d * d, axis=-1, keepdims=True)
    return d * lax.rsqrt(var + LN_EPS) * gamma + beta


def _pack_kernel(w_ref, o_ref):
    o_ref[...] = pltpu.bitcast(w_ref[...].astype(jnp.bfloat16), jnp.uint32)


def _pack_weight(w):
    k, n = w.shape
    rows = min(k, PACK_ROWS)
    cols = min(n, PACK_COLS)
    return pl.pallas_call(
        _pack_kernel,
        grid=(k // rows, n // cols),
        in_specs=[pl.BlockSpec((rows, cols), lambda i, j: (i, j))],
        out_specs=pl.BlockSpec((rows // 2, cols), lambda i, j: (i, j)),
        out_shape=jax.ShapeDtypeStruct((k // 2, n), jnp.uint32),
        name="pack_weight",
    )(w)


def _token(v):
    return (pltpu.bitcast(v, jnp.uint32) >> 16) >> 16


def _after(a, token):
    if token is None:
        return a
    head = a[0:SUBLANES, 0:LANES]
    if a.dtype == jnp.uint32:
        head = head | token
    else:
        head = pltpu.bitcast(pltpu.bitcast(head, jnp.uint32) | token, a.dtype)
    if a.shape[1] > LANES:
        head = jnp.concatenate([head, a[0:SUBLANES, LANES:]], axis=1)
    if a.shape[0] > SUBLANES:
        head = jnp.concatenate([head, a[SUBLANES:, :]], axis=0)
    return head


def _bf16_dot(a, w_u32):
    return jnp.dot(a.astype(jnp.bfloat16), pltpu.bitcast(w_u32, jnp.bfloat16), preferred_element_type=jnp.float32)


def _mixer_kernel(x_ref, w_in_ref, wa_ref, w_out_a_ref, wb8_ref, bias_b_ref, g_b_ref, b_b_ref,
                  w_out_b_ref, w_o_ref, g1_ref, b1_ref, o_ref, cv_ext, u_slab, conv_slab):
    t, d = x_ref.shape
    n_lane_groups = d // LANES
    i = pl.program_id(1)

    @pl.when(i == 0)
    def _():
        cv_ext[0:CV_HALO, :] = jnp.zeros((CV_HALO, d), jnp.float32)
        u_slab[:, 0:U_HALO, :] = jnp.zeros((n_lane_groups, U_HALO, LANES), jnp.float32)

    @pl.when(i > 0)
    def _():
        cv_ext[0:CV_HALO, :] = cv_ext[t:t + CV_HALO, :]
        u_slab[:, 0:U_HALO, :] = u_slab[:, t:t + U_HALO, :]

    x = x_ref[...]
    xb = x.astype(jnp.bfloat16)

    def proj(g, token=None, cols=slice(0, d)):
        w = _after(w_in_ref[:, g * d + cols.start:g * d + cols.stop], token)
        return jnp.dot(xb, pltpu.bitcast(w, jnp.bfloat16), preferred_element_type=jnp.float32)

    def compute_u(cols):
        u = proj(3, cols=cols) * jax.nn.sigmoid(proj(4, cols=cols))
        for g in range(cols.start // LANES, cols.stop // LANES):
            u_slab[g, U_HALO:U_HALO + t, :] = u[:, g * LANES - cols.start:(g + 1) * LANES - cols.start]
        return u

    def conv_block(r0, lg, token):
        lanes = slice(lg * LANES, (lg + 1) * LANES)
        half = CONV_ROWS // ROW_PHASES
        acc = [None] * ROW_PHASES
        for k in range(CONV_B_WIDTH):
            w_k = _after(wb8_ref[k * SUBLANES:(k + 1) * SUBLANES, lanes], token)
            for phase in range(ROW_PHASES):
                start = r0 + U_HALO - (CONV_B_WIDTH - 1) + k + phase
                u_k = u_slab[lg, pl.ds(start, half, stride=ROW_PHASES), :]
                term = u_k.reshape(half // SUBLANES, SUBLANES, LANES) * w_k[None]
                acc[phase] = term if acc[phase] is None else acc[phase] + term
        for phase in range(ROW_PHASES):
            conv_slab[lg, pl.ds(r0 + phase, half, stride=ROW_PHASES), :] = acc[phase].reshape(half, LANES)
        return _token(acc[-1][0])

    blocks = iter([(r0, lg) for lg in range(n_lane_groups) for r0 in range(0, t, CONV_ROWS)])
    group_tokens = []

    def conv_group(i, matmul_result):
        tok = _token(matmul_result[0:SUBLANES, 0:LANES])
        for _ in range(CONV_GROUPS[i]):
            item = next(blocks, None)
            if item is not None:
                tok = conv_block(*item, tok)
        group_tokens.append(tok)

    def pace(i):
        j = i - CONV_LAG
        return group_tokens[j] if 0 <= j < len(group_tokens) else None

    conv_group(0, compute_u(slice(0, d // 2)))
    conv_group(1, compute_u(slice(d // 2, d)))
    c_a = proj(1, pace(1))
    conv_group(2, c_a)
    v_a = proj(2, pace(2))
    conv_group(3, v_a)
    cv_ext[CV_HALO:CV_HALO + t, :] = c_a * v_a
    conv_a = None
    for k in range(CONV_A_WIDTH):
        off = CV_HALO - (CONV_A_WIDTH - 1) + k
        term = cv_ext[off:off + t, :] * wa_ref[k:k + 1, :]
        conv_a = term if conv_a is None else conv_a + term
    b_a = proj(0, pace(3))
    conv_group(4, b_a)
    g_a = proj(5, pace(4))
    conv_group(5, g_a)
    g_b = proj(6, pace(5))
    conv_group(6, g_b)
    y_a = _bf16_dot(_after(b_a * conv_a, pace(6)), w_out_a_ref[...])
    for item in blocks:
        conv_block(*item, None)
    gate_a = jax.nn.sigmoid(g_a) * y_a
    gate_b = jax.nn.sigmoid(g_b)

    for r0 in range(0, t, TAIL_ROWS):
        rows = slice(r0, min(r0 + TAIL_ROWS, t))
        conv = jnp.concatenate([conv_slab[g, rows, :] for g in range(n_lane_groups)], axis=1)
        ub = _layernorm(conv + bias_b_ref[...], g_b_ref[...], b_b_ref[...])
        y_b = _bf16_dot(ub * jax.nn.sigmoid(ub), w_out_b_ref[...])
        mix = _bf16_dot(gate_a[rows, :] + gate_b[rows, :] * y_b, w_o_ref[...])
        o_ref[rows, :] = _layernorm(ALPHA * x[rows, :] + mix, g1_ref[...], b1_ref[...])


def _mlp_kernel(x_ref, w_up_ref, w_down_ref, g2_ref, b2_ref, o_ref, *, ff_chunk):
    d_ff = w_up_ref.shape[1]
    t = x_ref.shape[0]
    for r0 in range(0, t, TAIL_ROWS):
        rows = slice(r0, min(r0 + TAIL_ROWS, t))
        x = x_ref[rows, :]
        xb = x.astype(jnp.bfloat16)
        ff = None
        for c in range(d_ff // ff_chunk):
            w_up_c = pltpu.bitcast(w_up_ref[:, c * ff_chunk:(c + 1) * ff_chunk], jnp.bfloat16)
            h = jnp.maximum(jnp.dot(xb, w_up_c, preferred_element_type=jnp.float32), 0.0)
            part = _bf16_dot(h * h, w_down_ref[c * ff_chunk // 2:(c + 1) * ff_chunk // 2, :])
            ff = part if ff is None else ff + part
        o_ref[rows, :] = _layernorm(ALPHA * x + ff, g2_ref[...], b2_ref[...])


def _resident(shape):
    return pl.BlockSpec(shape, lambda *_: (0,) * len(shape), pipeline_mode=pl.Buffered(1))


def _token_tile(n, largest):
    t = largest
    while t >= CONV_ROWS:
        if n % t == 0:
            return t
        t //= 2
    raise ValueError(f"token count {n} must be a multiple of {CONV_ROWS}")


def kernel(x, w_in, conv_a_w, w_out_a, conv_b_w, conv_b_bias, ln_b_gamma, ln_b_beta, w_out_b, w_o,
           ln1_gamma, ln1_beta, w_up, w_down, ln2_gamma, ln2_beta):
    b, s, d = x.shape
    d_ff = w_up.shape[1]
    assert w_in.shape == (d, N_PROJ * d) and conv_a_w.shape == (CONV_A_WIDTH, d)
    assert conv_b_w.shape == (CONV_B_WIDTH, d) and d % LANES == 0
    row = lambda v: v.reshape(1, d).astype(jnp.float32)
    t = _token_tile(s, MIXER_TILE)

    x1 = pl.pallas_call(
        _mixer_kernel,
        grid=(b, s // t),
        in_specs=[
            pl.BlockSpec((None, t, d), lambda bi, si: (bi, si, 0)),
            _resident((d // 2, N_PROJ * d)),
            _resident((CONV_A_WIDTH, d)),
            _resident((d // 2, d)),
            _resident((CONV_B_WIDTH * SUBLANES, d)),
            _resident((1, d)), _resident((1, d)), _resident((1, d)),
            _resident((d // 2, d)),
            _resident((d // 2, d)),
            _resident((1, d)), _resident((1, d)),
        ],
        out_specs=pl.BlockSpec((None, t, d), lambda bi, si: (bi, si, 0)),
        out_shape=jax.ShapeDtypeStruct((b, s, d), jnp.float32),
        scratch_shapes=[
            pltpu.VMEM((t + CV_HALO, d), jnp.float32),
            pltpu.VMEM((d // LANES, t + U_HALO, LANES), jnp.float32),
            pltpu.VMEM((d // LANES, t, LANES), jnp.float32),
        ],
        compiler_params=pltpu.CompilerParams(
            dimension_semantics=("arbitrary", "arbitrary"), vmem_limit_bytes=VMEM_LIMIT_BYTES),
        name="mixer",
    )(x, _pack_weight(w_in), conv_a_w, _pack_weight(w_out_a),
      jnp.repeat(conv_b_w, SUBLANES, axis=0), row(conv_b_bias), row(ln_b_gamma), row(ln_b_beta),
      _pack_weight(w_out_b), _pack_weight(w_o), row(ln1_gamma), row(ln1_beta))

    n = b * s
    tm = _token_tile(n, MLP_TILE)
    out = pl.pallas_call(
        functools.partial(_mlp_kernel, ff_chunk=d),
        grid=(n // tm,),
        in_specs=[
            pl.BlockSpec((tm, d), lambda m: (m, 0)),
            _resident((d // 2, d_ff)),
            _resident((d_ff // 2, d)),
            _resident((1, d)), _resident((1, d)),
        ],
        out_specs=pl.BlockSpec((tm, d), lambda m: (m, 0)),
        out_shape=jax.ShapeDtypeStruct((n, d), jnp.float32),
        compiler_params=pltpu.CompilerParams(
            dimension_semantics=("arbitrary",), vmem_limit_bytes=VMEM_LIMIT_BYTES),
        name="mlp",
    )(x1.reshape(n, d), _pack_weight(w_up), _pack_weight(w_down), row(ln2_gamma), row(ln2_beta))
    return out.reshape(b, s, d)
```

```python
import functools

import jax
import jax.numpy as jnp
from jax import lax
from jax.experimental import pallas as pl
from jax.experimental.pallas import tpu as pltpu

LN_EPS = 1e-5
ALPHA = 2.0 ** 0.25
CONV_A_WIDTH = 3
CONV_B_WIDTH = 31
N_PROJ = 7

SUBLANES = 8
LANES = 128
CV_HALO = 8
U_HALO = 32
CONV_ROWS = 128
ROW_PHASES = 2
TAIL_ROWS = 256
MIXER_TAIL_ROWS = 256
CONV_GROUPS = (2, 2, 2, 2, 2, 2, 2, 2)
CONV_LAG = 1
MIXER_TILE = 256
MLP_TILE = 512
PACK_ROWS, PACK_COLS = 512, 1024
VMEM_LIMIT_BYTES = 58 * 1024 * 1024


def _layernorm(y, gamma, beta):
    mu = jnp.mean(y, axis=-1, keepdims=True)
    d = y - mu
    var = jnp.mean(d * d, axis=-1, keepdims=True)
    return d * lax.rsqrt(var + LN_EPS) * gamma + beta


def _pack_kernel(w_ref, o_ref):
    o_ref[...] = pltpu.bitcast(w_ref[...].astype(jnp.bfloat16), jnp.uint32)


def _pack_weight(w):
    k, n = w.shape
    rows = min(k, PACK_ROWS)
    cols = min(n, PACK_COLS)
    return pl.pallas_call(
        _pack_kernel,
        grid=(k // rows, n // cols),
        in_specs=[pl.BlockSpec((rows, cols), lambda i, j: (i, j))],
        out_specs=pl.BlockSpec((rows // 2, cols), lambda i, j: (i, j)),
        out_shape=jax.ShapeDtypeStruct((k // 2, n), jnp.uint32),
        name="pack_weight",
    )(w)


def _token(v):
    return (pltpu.bitcast(v, jnp.uint32) >> 16) >> 16


def _after(a, token):
    if token is None:
        return a
    head = a[0:SUBLANES, 0:LANES]
    if a.dtype == jnp.uint32:
        head = head | token
    else:
        head = pltpu.bitcast(pltpu.bitcast(head, jnp.uint32) | token, a.dtype)
    if a.shape[1] > LANES:
        head = jnp.concatenate([head, a[0:SUBLANES, LANES:]], axis=1)
    if a.shape[0] > SUBLANES:
        head = jnp.concatenate([head, a[SUBLANES:, :]], axis=0)
    return head


def _bf16_dot(a, w_u32):
    return jnp.dot(a.astype(jnp.bfloat16), pltpu.bitcast(w_u32, jnp.bfloat16), preferred_element_type=jnp.float32)


def _mixer_step(cur, tile, refs, tiles_per_seq):
    (x_ref, x_prev_ref, w_in_ref, wa_ref, w_out_a_ref, wb8_ref, bias_b_ref, g_b_ref, b_b_ref,
     w_out_b_ref, w_o_ref, g1_ref, b1_ref, o_ref, cv_ext, u_slab, conv_slab, gate_a, gate_b) = refs
    t, d = x_ref.shape
    prev = 1 - cur
    n_lane_groups = d // LANES

    continues = lax.rem(tile, tiles_per_seq) != 0
    u_slab[cur, :, 0:U_HALO, :] = jnp.where(continues, u_slab[prev, :, t:t + U_HALO, :], 0.0)
    cv_ext[0:CV_HALO, :] = jnp.where(continues, cv_ext[t:t + CV_HALO, :], 0.0)

    x = x_ref[...]
    xb = x.astype(jnp.bfloat16)

    def proj(g, token=None, cols=slice(0, d)):
        w = _after(w_in_ref[:, g * d + cols.start:g * d + cols.stop], token)
        return jnp.dot(xb, pltpu.bitcast(w, jnp.bfloat16), preferred_element_type=jnp.float32)

    def compute_u(cols, token):
        u = proj(3, token, cols) * jax.nn.sigmoid(proj(4, None, cols))
        for g in range(cols.start // LANES, cols.stop // LANES):
            u_slab[cur, g, U_HALO:U_HALO + t, :] = u[:, g * LANES - cols.start:(g + 1) * LANES - cols.start]
        return u

    def conv_block(r0, lg, token):
        lanes = slice(lg * LANES, (lg + 1) * LANES)
        half = CONV_ROWS // ROW_PHASES
        acc = [None] * ROW_PHASES
        for k in range(CONV_B_WIDTH):
            w_k = _after(wb8_ref[k * SUBLANES:(k + 1) * SUBLANES, lanes], token)
            for phase in range(ROW_PHASES):
                start = r0 + U_HALO - (CONV_B_WIDTH - 1) + k + phase
                u_k = u_slab[prev, lg, pl.ds(start, half, stride=ROW_PHASES), :]
                term = u_k.reshape(half // SUBLANES, SUBLANES, LANES) * w_k[None]
                acc[phase] = term if acc[phase] is None else acc[phase] + term
        for phase in range(ROW_PHASES):
            conv_slab[lg, pl.ds(r0 + phase, half, stride=ROW_PHASES), :] = acc[phase].reshape(half, LANES)
        return _token(acc[-1][0])

    blocks = iter([(r0, lg) for lg in range(n_lane_groups) for r0 in range(0, t, CONV_ROWS)])
    group_tokens = []

    def conv_group(stage_result):
        tok = None if stage_result is None else _token(stage_result[0:SUBLANES, 0:LANES])
        for _ in range(CONV_GROUPS[len(group_tokens)]):
            item = next(blocks, None)
            if item is not None:
                tok = conv_block(*item, tok)
        group_tokens.append(tok)

    def pace(stage):
        g = stage - CONV_LAG
        return group_tokens[g] if 0 <= g < len(group_tokens) else None

    conv_group(None)
    u_lo = compute_u(slice(0, d // 2), pace(0))
    conv_group(u_lo)
    u_hi = compute_u(slice(d // 2, d), pace(1))
    conv_group(u_hi)
    c_a = proj(1, pace(2))
    conv_group(c_a)
    v_a = proj(2, pace(3))
    conv_group(v_a)
    cv_ext[CV_HALO:CV_HALO + t, :] = c_a * v_a
    conv_a = None
    for k in range(CONV_A_WIDTH):
        off = CV_HALO - (CONV_A_WIDTH - 1) + k
        term = cv_ext[off:off + t, :] * wa_ref[k:k + 1, :]
        conv_a = term if conv_a is None else conv_a + term
    b_a = proj(0, pace(4))
    conv_group(b_a)
    g_a = proj(5, pace(5))
    conv_group(g_a)
    g_b = proj(6, pace(6))
    conv_group(g_b)
    y_a = _bf16_dot(_after(b_a * conv_a, pace(7)), w_out_a_ref[...])
    for item in blocks:
        conv_block(*item, None)

    gate_a[cur] = jax.nn.sigmoid(g_a) * y_a
    gate_b[cur] = jax.nn.sigmoid(g_b)

    for r0 in range(0, t, MIXER_TAIL_ROWS):
        rows = slice(r0, min(r0 + MIXER_TAIL_ROWS, t))
        conv = jnp.concatenate([conv_slab[g, rows, :] for g in range(n_lane_groups)], axis=1)
        ub = _layernorm(conv + bias_b_ref[...], g_b_ref[...], b_b_ref[...])
        y_b = _bf16_dot(ub * jax.nn.sigmoid(ub), w_out_b_ref[...])
        mix = _bf16_dot(gate_a[prev, rows, :] + gate_b[prev, rows, :] * y_b, w_o_ref[...])
        o_ref[rows, :] = _layernorm(ALPHA * x_prev_ref[rows, :] + mix, g1_ref[...], b1_ref[...])


def _mixer_kernel(*refs, tiles_per_seq):
    j = pl.program_id(0)
    cv_ext, u_slab, _, gate_a, gate_b = refs[14:19]

    @pl.when(j == 0)
    def _():
        for ref in (cv_ext, u_slab, gate_a, gate_b):
            ref[...] = jnp.zeros(ref.shape, ref.dtype)

    for parity in range(2):
        @pl.when(lax.rem(j, 2) == parity)
        def _(parity=parity):
            _mixer_step(parity, j, refs, tiles_per_seq)


def _mlp_kernel(x_ref, w_up_ref, w_down_ref, g2_ref, b2_ref, o_ref, *, ff_chunk):
    d_ff = w_up_ref.shape[1]
    t = x_ref.shape[0]
    for r0 in range(0, t, TAIL_ROWS):
        rows = slice(r0, min(r0 + TAIL_ROWS, t))
        x = x_ref[rows, :]
        xb = x.astype(jnp.bfloat16)
        ff = None
        for c in range(d_ff // ff_chunk):
            w_up_c = pltpu.bitcast(w_up_ref[:, c * ff_chunk:(c + 1) * ff_chunk], jnp.bfloat16)
            h = jnp.maximum(jnp.dot(xb, w_up_c, preferred_element_type=jnp.float32), 0.0)
            part = _bf16_dot(h * h, w_down_ref[c * ff_chunk // 2:(c + 1) * ff_chunk // 2, :])
            ff = part if ff is None else ff + part
        o_ref[rows, :] = _layernorm(ALPHA * x + ff, g2_ref[...], b2_ref[...])


def _resident(shape):
    return pl.BlockSpec(shape, lambda *_: (0,) * len(shape), pipeline_mode=pl.Buffered(1))


def _token_tile(n, largest):
    t = largest
    while t >= CONV_ROWS:
        if n % t == 0:
            return t
        t //= 2
    raise ValueError(f"token count {n} must be a multiple of {CONV_ROWS}")


def kernel(x, w_in, conv_a_w, w_out_a, conv_b_w, conv_b_bias, ln_b_gamma, ln_b_beta, w_out_b, w_o,
           ln1_gamma, ln1_beta, w_up, w_down, ln2_gamma, ln2_beta):
    b, s, d = x.shape
    d_ff = w_up.shape[1]
    assert w_in.shape == (d, N_PROJ * d) and conv_a_w.shape == (CONV_A_WIDTH, d)
    assert conv_b_w.shape == (CONV_B_WIDTH, d) and d % (2 * LANES) == 0
    row = lambda v: v.reshape(1, d).astype(jnp.float32)
    n = b * s
    t = _token_tile(s, MIXER_TILE)
    n_tiles = n // t
    assert sum(CONV_GROUPS) <= (t // CONV_ROWS) * (d // LANES)
    x2 = x.reshape(n, d)

    x1 = pl.pallas_call(
        functools.partial(_mixer_kernel, tiles_per_seq=s // t),
        grid=(n_tiles + 1,),
        in_specs=[
            pl.BlockSpec((t, d), lambda j: (jnp.minimum(j, n_tiles - 1), 0)),
            pl.BlockSpec((t, d), lambda j: (jnp.maximum(j - 1, 0), 0)),
            _resident((d // 2, N_PROJ * d)),
            _resident((CONV_A_WIDTH, d)),
            _resident((d // 2, d)),
            _resident((CONV_B_WIDTH * SUBLANES, d)),
            _resident((1, d)), _resident((1, d)), _resident((1, d)),
            _resident((d // 2, d)),
            _resident((d // 2, d)),
            _resident((1, d)), _resident((1, d)),
        ],
        out_specs=pl.BlockSpec((t, d), lambda j: (jnp.maximum(j - 1, 0), 0)),
        out_shape=jax.ShapeDtypeStruct((n, d), jnp.float32),
        scratch_shapes=[
            pltpu.VMEM((t + CV_HALO, d), jnp.float32),
            pltpu.VMEM((2, d // LANES, t + U_HALO, LANES), jnp.float32),
            pltpu.VMEM((d // LANES, t, LANES), jnp.float32),
            pltpu.VMEM((2, t, d), jnp.float32),
            pltpu.VMEM((2, t, d), jnp.float32),
        ],
        compiler_params=pltpu.CompilerParams(
            dimension_semantics=("arbitrary",), vmem_limit_bytes=VMEM_LIMIT_BYTES),
        name="mixer",
    )(x2, x2, _pack_weight(w_in), conv_a_w, _pack_weight(w_out_a),
      jnp.repeat(conv_b_w, SUBLANES, axis=0), row(conv_b_bias), row(ln_b_gamma), row(ln_b_beta),
      _pack_weight(w_out_b), _pack_weight(w_o), row(ln1_gamma), row(ln1_beta))

    tm = _token_tile(n, MLP_TILE)
    out = pl.pallas_call(
        functools.partial(_mlp_kernel, ff_chunk=d),
        grid=(n // tm,),
        in_specs=[
            pl.BlockSpec((tm, d), lambda m: (m, 0)),
            _resident((d // 2, d_ff)),
            _resident((d_ff // 2, d)),
            _resident((1, d)), _resident((1, d)),
        ],
        out_specs=pl.BlockSpec((tm, d), lambda m: (m, 0)),
        out_shape=jax.ShapeDtypeStruct((n, d), jnp.float32),
        compiler_params=pltpu.CompilerParams(
            dimension_semantics=("arbitrary",), vmem_limit_bytes=VMEM_LIMIT_BYTES),
        name="mlp",
    )(x1, _pack_weight(w_up), _pack_weight(w_down), row(ln2_gamma), row(ln2_beta))
    return out.reshape(b, s, d)
```

```python
import functools

import jax
import jax.numpy as jnp
from jax import lax
from jax.experimental import pallas as pl
from jax.experimental.pallas import tpu as pltpu

LN_EPS = 1e-5
ALPHA = 2.0 ** 0.25
CONV_A_WIDTH = 3
CONV_B_WIDTH = 31
N_PROJ = 7

SUBLANES = 8
LANES = 128
CV_HALO = 8
U_HALO = 32
CONV_ROWS = 128
ROW_PHASES = 2
TAIL_ROWS = 256
MIXER_TAIL_ROWS = 256
CONV_GROUPS = (2, 2, 2, 2, 2, 2, 2, 2)
CONV_LAG = 1
MIXER_TILE = 256
MLP_TILE = 1024
PACK_ROWS, PACK_COLS = 512, 1024
VMEM_LIMIT_BYTES = 58 * 1024 * 1024


def _layernorm(y, gamma, beta):
    mu = jnp.mean(y, axis=-1, keepdims=True)
    d = y - mu
    var = jnp.mean(d * d, axis=-1, keepdims=True)
    return d * lax.rsqrt(var + LN_EPS) * gamma + beta


def _pack_kernel(w_ref, o_ref):
    o_ref[...] = pltpu.bitcast(w_ref[...].astype(jnp.bfloat16), jnp.uint32)


def _pack_weight(w):
    k, n = w.shape
    rows = min(k, PACK_ROWS)
    cols = min(n, PACK_COLS)
    return pl.pallas_call(
        _pack_kernel,
        grid=(k // rows, n // cols),
        in_specs=[pl.BlockSpec((rows, cols), lambda i, j: (i, j))],
        out_specs=pl.BlockSpec((rows // 2, cols), lambda i, j: (i, j)),
        out_shape=jax.ShapeDtypeStruct((k // 2, n), jnp.uint32),
        name="pack_weight",
    )(w)


def _token(v):
    return (pltpu.bitcast(v, jnp.uint32) >> 16) >> 16


def _after(a, token):
    if token is None:
        return a
    head = a[0:SUBLANES, 0:LANES]
    if a.dtype == jnp.uint32:
        head = head | token
    else:
        head = pltpu.bitcast(pltpu.bitcast(head, jnp.uint32) | token, a.dtype)
    if a.shape[1] > LANES:
        head = jnp.concatenate([head, a[0:SUBLANES, LANES:]], axis=1)
    if a.shape[0] > SUBLANES:
        head = jnp.concatenate([head, a[SUBLANES:, :]], axis=0)
    return head


def _bf16_dot(a, w_u32):
    return jnp.dot(a.astype(jnp.bfloat16), pltpu.bitcast(w_u32, jnp.bfloat16), preferred_element_type=jnp.float32)


def _mixer_step(cur, tile, refs, tiles_per_seq):
    (x_ref, x_prev_ref, w_in_ref, wa_ref, w_out_a_ref, wb8_ref, bias_b_ref, g_b_ref, b_b_ref,
     w_out_b_ref, w_o_ref, g1_ref, b1_ref, o_ref, cv_ext, u_slab, conv_slab, gate_a, gate_b) = refs
    t, d = x_ref.shape
    prev = 1 - cur
    n_lane_groups = d // LANES

    continues = lax.rem(tile, tiles_per_seq) != 0
    u_slab[cur, :, 0:U_HALO, :] = jnp.where(continues, u_slab[prev, :, t:t + U_HALO, :], 0.0)
    cv_ext[0:CV_HALO, :] = jnp.where(continues, cv_ext[t:t + CV_HALO, :], 0.0)

    x = x_ref[...]
    xb = x.astype(jnp.bfloat16)

    def proj(g, token=None, cols=slice(0, d)):
        w = _after(w_in_ref[:, g * d + cols.start:g * d + cols.stop], token)
        return jnp.dot(xb, pltpu.bitcast(w, jnp.bfloat16), preferred_element_type=jnp.float32)

    def compute_u(cols, token):
        u = proj(3, token, cols) * jax.nn.sigmoid(proj(4, None, cols))
        for g in range(cols.start // LANES, cols.stop // LANES):
            u_slab[cur, g, U_HALO:U_HALO + t, :] = u[:, g * LANES - cols.start:(g + 1) * LANES - cols.start]
        return u

    def conv_block(r0, lg, token):
        lanes = slice(lg * LANES, (lg + 1) * LANES)
        half = CONV_ROWS // ROW_PHASES
        acc = [None] * ROW_PHASES
        for k in range(CONV_B_WIDTH):
            w_k = _after(wb8_ref[k * SUBLANES:(k + 1) * SUBLANES, lanes], token)
            for phase in range(ROW_PHASES):
                start = r0 + U_HALO - (CONV_B_WIDTH - 1) + k + phase
                u_k = u_slab[prev, lg, pl.ds(start, half, stride=ROW_PHASES), :]
                term = u_k.reshape(half // SUBLANES, SUBLANES, LANES) * w_k[None]
                acc[phase] = term if acc[phase] is None else acc[phase] + term
        for phase in range(ROW_PHASES):
            conv_slab[lg, pl.ds(r0 + phase, half, stride=ROW_PHASES), :] = acc[phase].reshape(half, LANES)
        return _token(acc[-1][0])

    blocks = iter([(r0, lg) for lg in range(n_lane_groups) for r0 in range(0, t, CONV_ROWS)])
    group_tokens = []

    def conv_group(stage_result):
        tok = None if stage_result is None else _token(stage_result[0:SUBLANES, 0:LANES])
        for _ in range(CONV_GROUPS[len(group_tokens)]):
            item = next(blocks, None)
            if item is not None:
                tok = conv_block(*item, tok)
        group_tokens.append(tok)

    def pace(stage):
        g = stage - CONV_LAG
        return group_tokens[g] if 0 <= g < len(group_tokens) else None

    conv_group(None)
    u_lo = compute_u(slice(0, d // 2), pace(0))
    conv_group(u_lo)
    u_hi = compute_u(slice(d // 2, d), pace(1))
    conv_group(u_hi)
    c_a = proj(1, pace(2))
    conv_group(c_a)
    v_a = proj(2, pace(3))
    conv_group(v_a)
    cv_ext[CV_HALO:CV_HALO + t, :] = c_a * v_a
    conv_a = None
    for k in range(CONV_A_WIDTH):
        off = CV_HALO - (CONV_A_WIDTH - 1) + k
        term = cv_ext[off:off + t, :] * wa_ref[k:k + 1, :]
        conv_a = term if conv_a is None else conv_a + term
    b_a = proj(0, pace(4))
    conv_group(b_a)
    g_a = proj(5, pace(5))
    conv_group(g_a)
    g_b = proj(6, pace(6))
    conv_group(g_b)
    y_a = _bf16_dot(_after(b_a * conv_a, pace(7)), w_out_a_ref[...])
    for item in blocks:
        conv_block(*item, None)

    gate_a[cur] = jax.nn.sigmoid(g_a) * y_a
    gate_b[cur] = jax.nn.sigmoid(g_b)

    for r0 in range(0, t, MIXER_TAIL_ROWS):
        rows = slice(r0, min(r0 + MIXER_TAIL_ROWS, t))
        conv = jnp.concatenate([conv_slab[g, rows, :] for g in range(n_lane_groups)], axis=1)
        ub = _layernorm(conv + bias_b_ref[...], g_b_ref[...], b_b_ref[...])
        y_b = _bf16_dot(ub * jax.nn.sigmoid(ub), w_out_b_ref[...])
        mix = _bf16_dot(gate_a[prev, rows, :] + gate_b[prev, rows, :] * y_b, w_o_ref[...])
        o_ref[rows, :] = _layernorm(ALPHA * x_prev_ref[rows, :] + mix, g1_ref[...], b1_ref[...])


def _mixer_kernel(*refs, tiles_per_seq):
    j = pl.program_id(0)
    cv_ext, u_slab, _, gate_a, gate_b = refs[14:19]

    @pl.when(j == 0)
    def _():
        for ref in (cv_ext, u_slab, gate_a, gate_b):
            ref[...] = jnp.zeros(ref.shape, ref.dtype)

    for parity in range(2):
        @pl.when(lax.rem(j, 2) == parity)
        def _(parity=parity):
            _mixer_step(parity, j, refs, tiles_per_seq)


def _mlp_kernel(x_ref, w_up_ref, w_down_ref, g2_ref, b2_ref, o_ref, *, ff_chunk):
    d_ff = w_up_ref.shape[1]
    t = x_ref.shape[0]
    for r0 in range(0, t, TAIL_ROWS):
        rows = slice(r0, min(r0 + TAIL_ROWS, t))
        x = x_ref[rows, :]
        xb = x.astype(jnp.bfloat16)
        ff = None
        for c in range(d_ff // ff_chunk):
            w_up_c = pltpu.bitcast(w_up_ref[:, c * ff_chunk:(c + 1) * ff_chunk], jnp.bfloat16)
            h = jnp.maximum(jnp.dot(xb, w_up_c, preferred_element_type=jnp.float32), 0.0)
            part = _bf16_dot(h * h, w_down_ref[c * ff_chunk // 2:(c + 1) * ff_chunk // 2, :])
            ff = part if ff is None else ff + part
        o_ref[rows, :] = _layernorm(ALPHA * x + ff, g2_ref[...], b2_ref[...])


def _resident(shape):
    return pl.BlockSpec(shape, lambda *_: (0,) * len(shape), pipeline_mode=pl.Buffered(1))


def _token_tile(n, largest):
    t = largest
    while t >= CONV_ROWS:
        if n % t == 0:
            return t
        t //= 2
    raise ValueError(f"token count {n} must be a multiple of {CONV_ROWS}")


def kernel(x, w_in, conv_a_w, w_out_a, conv_b_w, conv_b_bias, ln_b_gamma, ln_b_beta, w_out_b, w_o,
           ln1_gamma, ln1_beta, w_up, w_down, ln2_gamma, ln2_beta):
    b, s, d = x.shape
    d_ff = w_up.shape[1]
    assert w_in.shape == (d, N_PROJ * d) and conv_a_w.shape == (CONV_A_WIDTH, d)
    assert conv_b_w.shape == (CONV_B_WIDTH, d) and d % (2 * LANES) == 0
    row = lambda v: v.reshape(1, d).astype(jnp.float32)
    n = b * s
    t = _token_tile(s, MIXER_TILE)
    n_tiles = n // t
    assert sum(CONV_GROUPS) <= (t // CONV_ROWS) * (d // LANES)
    x2 = x.reshape(n, d)

    x1 = pl.pallas_call(
        functools.partial(_mixer_kernel, tiles_per_seq=s // t),
        grid=(n_tiles + 1,),
        in_specs=[
            pl.BlockSpec((t, d), lambda j: (jnp.minimum(j, n_tiles - 1), 0)),
            pl.BlockSpec((t, d), lambda j: (jnp.maximum(j - 1, 0), 0)),
            _resident((d // 2, N_PROJ * d)),
            _resident((CONV_A_WIDTH, d)),
            _resident((d // 2, d)),
            _resident((CONV_B_WIDTH * SUBLANES, d)),
            _resident((1, d)), _resident((1, d)), _resident((1, d)),
            _resident((d // 2, d)),
            _resident((d // 2, d)),
            _resident((1, d)), _resident((1, d)),
        ],
        out_specs=pl.BlockSpec((t, d), lambda j: (jnp.maximum(j - 1, 0), 0)),
        out_shape=jax.ShapeDtypeStruct((n, d), jnp.float32),
        scratch_shapes=[
            pltpu.VMEM((t + CV_HALO, d), jnp.float32),
            pltpu.VMEM((2, d // LANES, t + U_HALO, LANES), jnp.float32),
            pltpu.VMEM((d // LANES, t, LANES), jnp.float32),
            pltpu.VMEM((2, t, d), jnp.float32),
            pltpu.VMEM((2, t, d), jnp.float32),
        ],
        compiler_params=pltpu.CompilerParams(
            dimension_semantics=("arbitrary",), vmem_limit_bytes=VMEM_LIMIT_BYTES),
        name="mixer",
    )(x2, x2, _pack_weight(w_in), conv_a_w, _pack_weight(w_out_a),
      jnp.repeat(conv_b_w, SUBLANES, axis=0), row(conv_b_bias), row(ln_b_gamma), row(ln_b_beta),
      _pack_weight(w_out_b), _pack_weight(w_o), row(ln1_gamma), row(ln1_beta))

    tm = _token_tile(n, MLP_TILE)
    out = pl.pallas_call(
        functools.partial(_mlp_kernel, ff_chunk=d),
        grid=(n // tm,),
        in_specs=[
            pl.BlockSpec((tm, d), lambda m: (m, 0)),
            _resident((d // 2, d_ff)),
            _resident((d_ff // 2, d)),
            _resident((1, d)), _resident((1, d)),
        ],
        out_specs=pl.BlockSpec((tm, d), lambda m: (m, 0)),
        out_shape=jax.ShapeDtypeStruct((n, d), jnp.float32),
        compiler_params=pltpu.CompilerParams(
            dimension_semantics=("arbitrary",), vmem_limit_bytes=VMEM_LIMIT_BYTES),
        name="mlp",
    )(x1, _pack_weight(w_up), _pack_weight(w_down), row(ln2_gamma), row(ln2_beta))
    return out.reshape(b, s, d)
```
